```python
import math
import jax
import jax.numpy as jnp
from jax import lax
import numpy as np

D_MODEL = 1024
BATCH = 2
SEQ = 8192
DEPTH = 2
DEC_BATCH = 128
DEC_SEQ = 4
PAST_LEN = 16384
PAGE_SIZE = 128

N_MIXERS = 2
N_SB_LAYERS = (DEPTH + 1) // 2
N_MLA_LAYERS = DEPTH // 2

SB_HEADS = 16
SB_KV_HEADS = 8
SB_GROUP = SB_HEADS // SB_KV_HEADS
SB_HEAD_DIM = D_MODEL // SB_HEADS

MLA_HEADS = 16
MLA_Q_RANK = 256
MLA_KV_RANK = 256
MLA_NOPE_DIM = 64
MLA_ROPE_DIM = 32
MLA_V_DIM = 64
MLA_QK_DIM = MLA_NOPE_DIM + MLA_ROPE_DIM
MLA_SCALE = MLA_QK_DIM ** -0.5
ROPE_THETA = 10000.0

D_FF = 2816
N_MOD = 9
MACARON_WEIGHT = 0.5

Q_BLOCK = 128
EPS = 1e-6
NEG_INF = -1e30

kernel_name = "sb_mla_macaron_adaln_decode_step"


def rms_norm(x):
    xf = x.astype(jnp.float32)
    return (xf * lax.rsqrt(jnp.mean(xf * xf, axis=-1, keepdims=True) + EPS)).astype(x.dtype)


def swiglu(u, w_in, w_out):
    gate, up = jnp.split(u @ w_in, 2, axis=-1)
    return (jax.nn.silu(gate) * up) @ w_out


def rope(x, pos):
    half = MLA_ROPE_DIM // 2
    inv_freq = jnp.exp(jnp.arange(half, dtype=jnp.float32) * (-math.log(ROPE_THETA) / half))
    ang = pos.astype(jnp.float32)[:, None] * inv_freq[None, :]
    ang = ang.reshape((pos.shape[0],) + (1,) * (x.ndim - 3) + (half,))
    cos = jnp.cos(ang).astype(x.dtype)
    sin = jnp.sin(ang).astype(x.dtype)
    x1, x2 = x[..., :half], x[..., half:]
    return jnp.concatenate([x1 * cos - x2 * sin, x1 * sin + x2 * cos], axis=-1)


def modulation(c, w_ada, b_ada):
    m = jax.nn.silu(c) @ w_ada + b_ada
    return m.reshape(c.shape[0], N_MOD, 1, D_MODEL)


def modulate(h, mod, j):
    return rms_norm(h) * (1 + mod[:, 3 * j + 1]) + mod[:, 3 * j]


def residual_add(h, out, mod, j, g_post, weight):
    return h + weight * (1 + mod[:, 3 * j + 2]) * (rms_norm(out) * g_post[j])


def ffn_sublayer(h, mod, j, g_post, w_in, w_out):
    return residual_add(h, swiglu(modulate(h, mod, j), w_in, w_out), mod, j, g_post, MACARON_WEIGHT)


def stick_breaking(q, k, v, readable):
    b, t, _, dh = q.shape
    qg = q.reshape(b, t, SB_KV_HEADS, SB_GROUP, dh)
    z = jnp.einsum("btgrd,bsgd->bgrts", qg, k).astype(jnp.float32) * (dh ** -0.5)
    log_beta = jax.nn.log_sigmoid(z)
    log_1m_beta = jnp.where(readable, jax.nn.log_sigmoid(-z), 0.0)
    tail = lax.cumsum(log_1m_beta, axis=4, reverse=True) - log_1m_beta
    a = jnp.where(readable, jnp.exp(log_beta + tail), 0.0)
    o = jnp.einsum("bgrts,bsgd->btgrd", a.astype(v.dtype), v)
    return o.reshape(b, t, SB_HEADS * dh)


def sb_qkv(u, w_qkv):
    b, s, _ = u.shape
    nq = SB_HEADS * SB_HEAD_DIM
    nk = SB_KV_HEADS * SB_HEAD_DIM
    qkv = u @ w_qkv
    q = qkv[..., :nq].reshape(b, s, SB_HEADS, SB_HEAD_DIM)
    k = qkv[..., nq:nq + nk].reshape(b, s, SB_KV_HEADS, SB_HEAD_DIM)
    v = qkv[..., nq + nk:].reshape(b, s, SB_KV_HEADS, SB_HEAD_DIM)
    return q, k, v


def sb_prompt(u, w_qkv, w_o):
    b, s, _ = u.shape
    q, k, v = sb_qkv(u, w_qkv)
    n_blk = s // Q_BLOCK
    key_pos = jnp.arange(s)
    qb = jnp.moveaxis(q.reshape(b, n_blk, Q_BLOCK, SB_HEADS, SB_HEAD_DIM), 1, 0)

    def block(args):
        qi, i = args
        qpos = i * Q_BLOCK + jnp.arange(Q_BLOCK)
        readable = key_pos[None, :] < qpos[:, None]
        return stick_breaking(qi, k, v, readable)

    o = lax.map(block, (qb, jnp.arange(n_blk)))
    o = jnp.moveaxis(o, 0, 1).reshape(b, s, SB_HEADS * SB_HEAD_DIM)
    return o @ w_o, k, v


def sb_sample(u, cache_k, cache_v, page_table, w_qkv, w_o):
    b, t, _ = u.shape
    q, k, v = sb_qkv(u, w_qkv)
    past = page_table.shape[1] * PAGE_SIZE
    key_pos = jnp.arange(past + t)
    qpos = past + jnp.arange(t)
    readable = key_pos[None, :] < qpos[:, None]

    def one(args):
        qi, ki, vi, pages = args
        kk = jnp.concatenate([cache_k[pages].reshape(past, SB_KV_HEADS, SB_HEAD_DIM), ki], axis=0)
        vv = jnp.concatenate([cache_v[pages].reshape(past, SB_KV_HEADS, SB_HEAD_DIM), vi], axis=0)
        return stick_breaking(qi[None], kk[None], vv[None], readable)[0]

    o = lax.map(one, (q, k, v, page_table))
    return o @ w_o, k, v


def mla_project(u, pos, w_dq, g_q, w_uq, w_dkv, g_kv):
    b, s, _ = u.shape
    q = ((rms_norm(u @ w_dq) * g_q) @ w_uq).reshape(b, s, MLA_HEADS, MLA_QK_DIM)
    q_nope = q[..., :MLA_NOPE_DIM]
    q_rope = rope(q[..., MLA_NOPE_DIM:], pos)
    kv = u @ w_dkv
    ckv = rms_norm(kv[..., :MLA_KV_RANK]) * g_kv
    k_rope = rope(kv[..., MLA_KV_RANK:], pos)
    return q_nope, q_rope, ckv, k_rope


def mla_prompt(u, w_dq, g_q, w_uq, w_dkv, g_kv, w_uk, w_uv, w_o):
    b, s, _ = u.shape
    pos = jnp.arange(s)
    q_nope, q_rope, ckv, k_rope = mla_project(u, pos, w_dq, g_q, w_uq, w_dkv, g_kv)
    k_nope = jnp.einsum("bsc,chn->bshn", ckv, w_uk)
    v = jnp.einsum("bsc,chv->bshv", ckv, w_uv)
    n_blk = s // Q_BLOCK

    def to_blocks(a):
        return jnp.moveaxis(a.reshape((b, n_blk, Q_BLOCK) + a.shape[2:]), 1, 0)

    def block(args):
        qn, qr, i = args
        qpos = i * Q_BLOCK + jnp.arange(Q_BLOCK)
        visible = pos[None, :] <= qpos[:, None]
        logits = (jnp.einsum("bthn,bshn->bhts", qn, k_nope)
                  + jnp.einsum("bthr,bsr->bhts", qr, k_rope)).astype(jnp.float32) * MLA_SCALE
        p = jax.nn.softmax(jnp.where(visible, logits, NEG_INF), axis=-1)
        return jnp.einsum("bhts,bshv->bthv", p.astype(v.dtype), v)

    o = lax.map(block, (to_blocks(q_nope), to_blocks(q_rope), jnp.arange(n_blk)))
    o = jnp.moveaxis(o, 0, 1).reshape(b, s, MLA_HEADS * MLA_V_DIM)
    return o @ w_o, ckv, k_rope


def mla_sample(u, cache_ckv, cache_krope, page_table, w_dq, g_q, w_uq, w_dkv, g_kv, w_uk, w_uv, w_o):
    b, t, _ = u.shape
    past = page_table.shape[1] * PAGE_SIZE
    pos = past + jnp.arange(t)
    q_nope, q_rope, ckv, k_rope = mla_project(u, pos, w_dq, g_q, w_uq, w_dkv, g_kv)
    q_lat = jnp.einsum("bthn,chn->bthc", q_nope, w_uk)
    key_pos = jnp.arange(past + t)
    visible = key_pos[None, :] <= pos[:, None]

    def one(args):
        ql, qr, c_new, kr_new, pages = args
        c_all = jnp.concatenate([cache_ckv[pages].reshape(past, MLA_KV_RANK), c_new], axis=0)
        kr_all = jnp.concatenate([cache_krope[pages].reshape(past, MLA_ROPE_DIM), kr_new], axis=0)
        logits = (jnp.einsum("thc,sc->hts", ql, c_all)
                  + jnp.einsum("thr,sr->hts", qr, kr_all)).astype(jnp.float32) * MLA_SCALE
        p = jax.nn.softmax(jnp.where(visible, logits, NEG_INF), axis=-1)
        return jnp.einsum("hts,sc->thc", p.astype(c_all.dtype), c_all)

    o_lat = lax.map(one, (q_lat, q_rope, ckv, k_rope, page_table))
    o = jnp.einsum("bthc,chv->bthv", o_lat, w_uv).reshape(b, t, MLA_HEADS * MLA_V_DIM)
    return o @ w_o, ckv, k_rope


def setup_inputs(seed: int = 0) -> dict:
    key = jax.random.key(seed)
    ks = iter(jax.random.split(key, 32))

    def nrm(shape, scale):
        return jax.random.normal(next(ks), shape, jnp.float32) * scale

    n_pages = PAST_LEN // PAGE_SIZE
    n_used = DEC_BATCH * n_pages
    n_pool = (5 * n_used + 3) // 4
    page_table = jax.random.permutation(next(ks), n_pool)[:n_used].reshape(DEC_BATCH, n_pages).astype(jnp.int32)
    sb_qkv_width = (SB_HEADS + 2 * SB_KV_HEADS) * SB_HEAD_DIM
    return {
        "x_prompt": nrm((BATCH, SEQ, D_MODEL), 1.0),
        "x_sample": nrm((DEC_BATCH, DEC_SEQ, D_MODEL), 1.0),
        "c_prompt": nrm((BATCH, D_MODEL), 1.0),
        "c_sample": nrm((DEC_BATCH, D_MODEL), 1.0),
        "cache_sb_k": nrm((N_SB_LAYERS, n_pool, PAGE_SIZE, SB_KV_HEADS, SB_HEAD_DIM), 1.0),
        "cache_sb_v": nrm((N_SB_LAYERS, n_pool, PAGE_SIZE, SB_KV_HEADS, SB_HEAD_DIM), 1.0),
        "cache_mla_ckv": nrm((N_MLA_LAYERS, n_pool, PAGE_SIZE, MLA_KV_RANK), 1.0),
        "cache_mla_krope": nrm((N_MLA_LAYERS, n_pool, PAGE_SIZE, MLA_ROPE_DIM), 1.0),
        "page_table": page_table,
        "w_ada": nrm((DEPTH, D_MODEL, N_MOD * D_MODEL), 0.2 * D_MODEL ** -0.5),
        "b_ada": nrm((DEPTH, N_MOD * D_MODEL), 0.01),
        "g_post": 1.0 + nrm((DEPTH, 3, D_MODEL), 0.01),
        "w_ffn_in": nrm((DEPTH, 2, D_MODEL, 2 * D_FF), D_MODEL ** -0.5),
        "w_ffn_out": nrm((DEPTH, 2, D_FF, D_MODEL), D_FF ** -0.5),
        "sb_w_qkv": nrm((N_SB_LAYERS, D_MODEL, sb_qkv_width), D_MODEL ** -0.5),
        "sb_w_o": nrm((N_SB_LAYERS, SB_HEADS * SB_HEAD_DIM, D_MODEL), (SB_HEADS * SB_HEAD_DIM) ** -0.5),
        "mla_w_dq": nrm((N_MLA_LAYERS, D_MODEL, MLA_Q_RANK), D_MODEL ** -0.5),
        "mla_g_q": 1.0 + nrm((N_MLA_LAYERS, MLA_Q_RANK), 0.01),
        "mla_w_uq": nrm((N_MLA_LAYERS, MLA_Q_RANK, MLA_HEADS * MLA_QK_DIM), MLA_Q_RANK ** -0.5),
        "mla_w_dkv": nrm((N_MLA_LAYERS, D_MODEL, MLA_KV_RANK + MLA_ROPE_DIM), D_MODEL ** -0.5),
        "mla_g_kv": 1.0 + nrm((N_MLA_LAYERS, MLA_KV_RANK), 0.01),
        "mla_w_uk": nrm((N_MLA_LAYERS, MLA_KV_RANK, MLA_HEADS, MLA_NOPE_DIM), MLA_KV_RANK ** -0.5),
        "mla_w_uv": nrm((N_MLA_LAYERS, MLA_KV_RANK, MLA_HEADS, MLA_V_DIM), MLA_KV_RANK ** -0.5),
        "mla_w_o": nrm((N_MLA_LAYERS, MLA_HEADS * MLA_V_DIM, D_MODEL), (MLA_HEADS * MLA_V_DIM) ** -0.5),
    }


def reference(x_prompt, x_sample, c_prompt, c_sample, cache_sb_k, cache_sb_v, cache_mla_ckv,
              cache_mla_krope, page_table, w_ada, b_ada, g_post, w_ffn_in, w_ffn_out, sb_w_qkv,
              sb_w_o, mla_w_dq, mla_g_q, mla_w_uq, mla_w_dkv, mla_g_kv, mla_w_uk, mla_w_uv, mla_w_o):
    hp, hs = x_prompt, x_sample
    sbk_p, sbv_p, ckv_p, kr_p = [], [], [], []
    sbk_s, sbv_s, ckv_s, kr_s = [], [], [], []
    for layer in range(DEPTH):
        mod_p = modulation(c_prompt, w_ada[layer], b_ada[layer])
        mod_s = modulation(c_sample, w_ada[layer], b_ada[layer])
        g = g_post[layer]
        hp = ffn_sublayer(hp, mod_p, 0, g, w_ffn_in[layer, 0], w_ffn_out[layer, 0])
        hs = ffn_sublayer(hs, mod_s, 0, g, w_ffn_in[layer, 0], w_ffn_out[layer, 0])
        up = modulate(hp, mod_p, 1)
        us = modulate(hs, mod_s, 1)
        idx = layer // N_MIXERS
        if layer % N_MIXERS == 0:
            mp, k_p, v_p = sb_prompt(up, sb_w_qkv[idx], sb_w_o[idx])
            ms, k_s, v_s = sb_sample(us, cache_sb_k[idx], cache_sb_v[idx], page_table,
                                     sb_w_qkv[idx], sb_w_o[idx])
            sbk_p.append(k_p)
            sbv_p.append(v_p)
            sbk_s.append(k_s)
            sbv_s.append(v_s)
        else:
            mla_w = (mla_w_dq[idx], mla_g_q[idx], mla_w_uq[idx], mla_w_dkv[idx], mla_g_kv[idx],
                     mla_w_uk[idx], mla_w_uv[idx], mla_w_o[idx])
            mp, c_p, r_p = mla_prompt(up, *mla_w)
            ms, c_s, r_s = mla_sample(us, cache_mla_ckv[idx], cache_mla_krope[idx], page_table, *mla_w)
            ckv_p.append(c_p)
            kr_p.append(r_p)
            ckv_s.append(c_s)
            kr_s.append(r_s)
        hp = residual_add(hp, mp, mod_p, 1, g, 1.0)
        hs = residual_add(hs, ms, mod_s, 1, g, 1.0)
        hp = ffn_sublayer(hp, mod_p, 2, g, w_ffn_in[layer, 1], w_ffn_out[layer, 1])
        hs = ffn_sublayer(hs, mod_s, 2, g, w_ffn_in[layer, 1], w_ffn_out[layer, 1])
    sb_k_prompt = jnp.stack(sbk_p)
    sb_v_prompt = jnp.stack(sbv_p)
    mla_ckv_prompt = jnp.stack(ckv_p)
    mla_krope_prompt = jnp.stack(kr_p)
    sb_k_sample = jnp.stack(sbk_s)
    sb_v_sample = jnp.stack(sbv_s)
    mla_ckv_sample = jnp.stack(ckv_s)
    mla_krope_sample = jnp.stack(kr_s)
    return (hp, hs, sb_k_prompt, sb_v_prompt, mla_ckv_prompt, mla_krope_prompt,
            sb_k_sample, sb_v_sample, mla_ckv_sample, mla_krope_sample)
```

```python
import functools
import math

import numpy as np
import jax
import jax.numpy as jnp
from jax import lax
from jax.experimental import pallas as pl
from jax.experimental.pallas import tpu as pltpu

F32 = jnp.float32
BF16 = jnp.bfloat16

EPS = 1e-6
NEG_INF = -1e30
ROPE_THETA = 10000.0
MACARON_WEIGHT = 0.5
N_MOD = 9

HEAD_PAD = 128
VMEM_LIMIT_BYTES = 56 * 1024 * 1024
FFN_TOKENS = 512
PROJ_TOKENS = 256
SB_Q_TILE = 256
MLA_Q_TILE = 512
MLA_PAGES_PER_CHUNK = 8
SB_TAIL_EXIT = 110.0

_NT = (((1,), (1,)), ((), ()))


def _cparams(*sem):
    return pltpu.CompilerParams(dimension_semantics=sem, vmem_limit_bytes=VMEM_LIMIT_BYTES)


def _const_spec(shape):
    zeros = (0,) * len(shape)
    return pl.BlockSpec(shape, lambda *_: zeros, pipeline_mode=pl.Buffered(1))


def _rms(x):
    return x * lax.rsqrt(jnp.mean(x * x, axis=-1, keepdims=True) + EPS)


def _dot(a, b):
    return jnp.dot(a, b, preferred_element_type=F32)


def _silu(x):
    return x * jax.nn.sigmoid(x)


def _mod_kernel(c_ref, w_ref, b_ref, o_ref):
    a = _silu(c_ref[...]).astype(BF16)
    o_ref[...] = _dot(a, w_ref[...].astype(BF16)) + b_ref[...]


def _modulation(c_all, w_ada, b_ada):
    depth, d, n = w_ada.shape
    rows = c_all.shape[0]
    tn = 1024
    return pl.pallas_call(
        _mod_kernel,
        grid=(depth, n // tn),
        in_specs=[
            pl.BlockSpec((rows, d), lambda l, j: (0, 0)),
            pl.BlockSpec((None, d, tn), lambda l, j: (l, 0, j)),
            pl.BlockSpec((None, 1, tn), lambda l, j: (l, 0, j)),
        ],
        out_specs=pl.BlockSpec((None, rows, tn), lambda l, j: (l, 0, j)),
        out_shape=jax.ShapeDtypeStruct((depth, rows, n), F32),
        compiler_params=_cparams("parallel", "parallel"),
        name="modulation",
    )(c_all, w_ada, b_ada.reshape(depth, 1, n))


def _token_call(body, h3, mod4, j, extra, extra_specs, out_widths, out_dtypes, tm, name):
    g, t, d = h3.shape
    tm = min(tm, t)
    per_token = mod4.shape[2] != 1
    rb = tm if per_token else 1
    mod_spec = pl.BlockSpec((None, 3, rb, d), lambda gi, i: (gi, j, i if per_token else 0, 0))
    tok_spec = pl.BlockSpec((None, tm, d), lambda gi, i: (gi, i, 0))
    out_specs = [pl.BlockSpec((None, tm, w), lambda gi, i: (gi, i, 0)) for w in out_widths]
    out_shape = [jax.ShapeDtypeStruct((g, t, w), dt) for w, dt in zip(out_widths, out_dtypes)]
    return pl.pallas_call(
        body,
        grid=(g, t // tm),
        in_specs=[tok_spec, mod_spec] + list(extra_specs),
        out_specs=out_specs,
        out_shape=out_shape,
        compiler_params=_cparams("parallel", "parallel"),
        name=name,
    )(h3, mod4, *extra)


def _modulate(h_ref, m_ref):
    return (_rms(h_ref[...]) * (1.0 + m_ref[1]) + m_ref[0]).astype(BF16)


def _ffn_kernel(h_ref, m_ref, gp_ref, win_ref, wout_ref, o_ref, *, ff, ck):
    x = h_ref[...]
    u = _modulate(h_ref, m_ref)
    acc = None
    for c in range(ff // ck):
        gate = _dot(u, win_ref[:, c * ck:(c + 1) * ck])
        up = _dot(u, win_ref[:, ff + c * ck:ff + (c + 1) * ck])
        act = (_silu(gate) * up).astype(BF16)
        y = _dot(act, wout_ref[c * ck:(c + 1) * ck, :])
        acc = y if acc is None else acc + y
    o_ref[...] = x + MACARON_WEIGHT * (1.0 + m_ref[2]) * (_rms(acc) * gp_ref[...])


def _ffn(h3, mod4, j, gp, w_in, w_out, tm):
    ff = w_out.shape[0]
    ck = ff // 2 if (ff // 2) % HEAD_PAD == 0 else ff
    body = functools.partial(_ffn_kernel, ff=ff, ck=ck)
    d = h3.shape[-1]
    (out,) = _token_call(
        body, h3, mod4, j, [gp.reshape(1, d), w_in, w_out],
        [_const_spec((1, d)), _const_spec(w_in.shape), _const_spec(w_out.shape)],
        [d], [F32], tm, "ffn")
    return out


def _proj_kernel(h_ref, m_ref, w_ref, *out_refs, widths):
    y = _dot(_modulate(h_ref, m_ref), w_ref[...])
    start = 0
    for o_ref, w in zip(out_refs, widths):
        o_ref[...] = y[:, start:start + w].astype(o_ref.dtype)
        start += w


def _proj(h3, mod4, j, w, widths, dtypes, tm):
    body = functools.partial(_proj_kernel, widths=tuple(widths))
    return _token_call(body, h3, mod4, j, [w], [_const_spec(w.shape)], widths, dtypes, tm, "proj")


def _oproj_kernel(h_ref, m_ref, o_ref, gp_ref, w_ref, out_ref, *, weight):
    y = _rms(_dot(o_ref[...], w_ref[...])) * gp_ref[...]
    out_ref[...] = h_ref[...] + weight * (1.0 + m_ref[2]) * y


def _oproj(h3, mod4, j, o3, gp, w, tm):
    d = h3.shape[-1]
    k = o3.shape[-1]
    body = functools.partial(_oproj_kernel, weight=1.0)
    (out,) = _token_call(
        body, h3, mod4, j, [o3, gp.reshape(1, d), w],
        [pl.BlockSpec((None, tm, k), lambda gi, i: (gi, i, 0)), _const_spec((1, d)), _const_spec(w.shape)],
        [d], [F32], tm, "oproj")
    return out


def _oproj_lat_kernel(h_ref, m_ref, o_ref, gp_ref, wuv_ref, wo_ref, out_ref, *, heads, rank):
    parts = []
    for hh in range(heads):
        parts.append(_dot(o_ref[:, hh * rank:(hh + 1) * rank], wuv_ref[hh]).astype(BF16))
    o_pad = jnp.concatenate(parts, axis=1)
    y = _rms(_dot(o_pad, wo_ref[...])) * gp_ref[...]
    out_ref[...] = h_ref[...] + (1.0 + m_ref[2]) * y


def _oproj_lat(h3, mod4, j, o3, gp, wuv, wo, tm):
    d = h3.shape[-1]
    heads, rank, _ = wuv.shape
    body = functools.partial(_oproj_lat_kernel, heads=heads, rank=rank)
    (out,) = _token_call(
        body, h3, mod4, j, [o3, gp.reshape(1, d), wuv, wo],
        [pl.BlockSpec((None, tm, heads * rank), lambda gi, i: (gi, i, 0)), _const_spec((1, d)),
         _const_spec(wuv.shape), _const_spec(wo.shape)],
        [d], [F32], tm, "oproj_lat")
    return out


def _sb_weights(z, u, car, readable):
    ts = z.shape[1]
    e = jnp.exp(-jnp.abs(z))
    log_beta = jnp.minimum(z, 0.0) - jnp.log1p(e)
    l1m = log_beta - z
    if readable is not None:
        l1m = jnp.where(readable, l1m, 0.0)
    hi = l1m.astype(BF16)
    lo = (l1m - hi.astype(F32)).astype(BF16)
    t = _dot(hi, u) + _dot(lo, u)
    car_b = car if ts == HEAD_PAD else jnp.concatenate([car] * (ts // HEAD_PAD), axis=1)
    a = jnp.exp(log_beta + t[:, :ts] + car_b)
    if readable is not None:
        a = jnp.where(readable, a, 0.0)
    return a, car + t[:, ts:]


def _sb_attn_kernel(q_ref, k_ref, v_ref, u_ref, o_ref, acc_ref, car_ref, *, tq, grp):
    qi = pl.program_id(2)
    rows = grp * tq
    q = jnp.concatenate([q_ref[:, r * HEAD_PAD:(r + 1) * HEAD_PAD] for r in range(grp)], axis=0)
    u = u_ref[...]

    def block(kb, masked):
        start = pl.multiple_of(kb * tq, tq)
        z = lax.dot_general(q, k_ref[pl.ds(start, tq), :], _NT, preferred_element_type=F32)
        readable = None
        if masked:
            row = lax.broadcasted_iota(jnp.int32, (rows, tq), 0)
            col = lax.broadcasted_iota(jnp.int32, (rows, tq), 1)
            readable = col < lax.rem(row, tq)
        a, car = _sb_weights(z, u, car_ref[...], readable)
        acc_ref[...] += _dot(a.astype(BF16), v_ref[pl.ds(start, tq), :])
        car_ref[...] = car
        return jnp.max(car)

    acc_ref[...] = jnp.zeros_like(acc_ref)
    car_ref[...] = jnp.zeros_like(car_ref)
    top = block(qi, True)

    def cond(c):
        return jnp.logical_and(c[0] >= 0, c[1] > -SB_TAIL_EXIT)

    def body(c):
        return c[0] - 1, block(c[0], False)

    lax.while_loop(cond, body, (qi - 1, top))
    acc = acc_ref[...]
    o_ref[...] = jnp.concatenate([acc[r * tq:(r + 1) * tq] for r in range(grp)], axis=1).astype(o_ref.dtype)


def _prefix_matrix(ts):
    j = np.arange(ts)[:, None]
    s = np.arange(ts)[None, :]
    later = (j > s).astype(np.float32)
    return jnp.asarray(np.concatenate([later, np.ones((ts, HEAD_PAD), np.float32)], axis=1), dtype=BF16)


def _sb_attention(q, k, v, kvh, tq):
    b, s, hq = q.shape
    tq = min(tq, s)
    grp = hq // (kvh * HEAD_PAD)
    u = _prefix_matrix(tq)
    body = functools.partial(_sb_attn_kernel, tq=tq, grp=grp)
    return pl.pallas_call(
        body,
        grid=(b, kvh, s // tq),
        in_specs=[
            pl.BlockSpec((None, tq, grp * HEAD_PAD), lambda bi, g, i: (bi, i, g)),
            pl.BlockSpec((None, s, HEAD_PAD), lambda bi, g, i: (bi, 0, g)),
            pl.BlockSpec((None, s, HEAD_PAD), lambda bi, g, i: (bi, 0, g)),
            _const_spec(u.shape),
        ],
        out_specs=pl.BlockSpec((None, tq, grp * HEAD_PAD), lambda bi, g, i: (bi, i, g)),
        out_shape=jax.ShapeDtypeStruct((b, s, hq), BF16),
        scratch_shapes=[pltpu.VMEM((grp * tq, HEAD_PAD), F32), pltpu.VMEM((grp * tq, HEAD_PAD), F32)],
        compiler_params=_cparams("parallel", "parallel", "arbitrary"),
        name="sb_attention",
    )(q, k, v, u)


def _sb_decode_kernel(pt_ref, q_ref, kn_ref, vn_ref, u_ref, kc_ref, vc_ref, o_ref,
                      kbuf, vbuf, sem, acc_ref, car_ref, *, layer, n_pages, kvh, rpg, dec_t):
    b = pl.program_id(0)
    rows = kvh * rpg
    page_len = kbuf.shape[-1]

    def copies(p):
        slot = lax.rem(n_pages - 1 - p, 2)
        pg = pt_ref[b, p]
        return (pltpu.make_async_copy(kc_ref.at[layer, pg], kbuf.at[slot], sem.at[0, slot]),
                pltpu.make_async_copy(vc_ref.at[layer, pg], vbuf.at[slot], sem.at[1, slot]))

    def start(p):
        for c in copies(p):
            c.start()

    def wait(p):
        for c in copies(p):
            c.wait()

    start(n_pages - 1)
    start(n_pages - 2)
    u = u_ref[...]

    def block(kt, vt, readable):
        z = jnp.concatenate(
            [_dot(q_ref[g].astype(BF16), kt(g).astype(BF16)) for g in range(kvh)], axis=0)
        a, car = _sb_weights(z, u, car_ref[...], readable)
        a = a.astype(BF16)
        for g in range(kvh):
            acc_ref[g] += lax.dot_general(a[g * rpg:(g + 1) * rpg], vt(g).astype(BF16), _NT,
                                          preferred_element_type=F32)
        car_ref[...] = car
        return jnp.max(car)

    acc_ref[...] = jnp.zeros_like(acc_ref)
    car_ref[...] = jnp.zeros_like(car_ref)
    row = lax.broadcasted_iota(jnp.int32, (rows, page_len), 0)
    col = lax.broadcasted_iota(jnp.int32, (rows, page_len), 1)
    top = block(lambda g: kn_ref[g], lambda g: vn_ref[g], col < lax.rem(row, dec_t))

    def cond(c):
        return jnp.logical_and(c[0] >= 0, c[1] > -SB_TAIL_EXIT)

    def body(c):
        p, _, lo = c
        slot = lax.rem(n_pages - 1 - p, 2)
        wait(p)
        mx = block(lambda g: kbuf[slot, g], lambda g: vbuf[slot, g], None)
        more = jnp.logical_and(p >= 2, mx > -SB_TAIL_EXIT)

        @pl.when(more)
        def _():
            start(p - 2)

        return p - 1, mx, jnp.where(more, p - 2, lo)

    p, _, lo = lax.while_loop(cond, body, (n_pages - 1, top, n_pages - 2))

    @pl.when(p >= lo)
    def _():
        wait(jnp.maximum(p, 0))

    @pl.when(p - 1 >= lo)
    def _():
        wait(jnp.maximum(p - 1, 0))

    o_ref[...] = acc_ref[...]


def _sb_decode(q, kn, vn, kc, vc, page_table, layer, dec_t):
    db, kvh, rpg, hd = q.shape
    n_pages = page_table.shape[1]
    page_len = kc.shape[-1]
    u = _prefix_matrix(page_len)
    body = functools.partial(_sb_decode_kernel, layer=layer, n_pages=n_pages, kvh=kvh, rpg=rpg, dec_t=dec_t)
    grid_spec = pltpu.PrefetchScalarGridSpec(
        num_scalar_prefetch=1,
        grid=(db,),
        in_specs=[
            pl.BlockSpec((None, kvh, rpg, hd), lambda bi, pt: (bi, 0, 0, 0)),
            pl.BlockSpec((None, kvh, hd, page_len), lambda bi, pt: (bi, 0, 0, 0)),
            pl.BlockSpec((None, kvh, hd, page_len), lambda bi, pt: (bi, 0, 0, 0)),
            pl.BlockSpec(u.shape, lambda bi, pt: (0, 0)),
            pl.BlockSpec(memory_space=pl.ANY),
            pl.BlockSpec(memory_space=pl.ANY),
        ],
        out_specs=pl.BlockSpec((None, kvh, rpg, hd), lambda bi, pt: (bi, 0, 0, 0)),
        scratch_shapes=[
            pltpu.VMEM((2, kvh, hd, page_len), F32),
            pltpu.VMEM((2, kvh, hd, page_len), F32),
            pltpu.SemaphoreType.DMA((2, 2)),
            pltpu.VMEM((kvh, rpg, hd), F32),
            pltpu.VMEM((kvh * rpg, HEAD_PAD), F32),
        ],
    )
    return pl.pallas_call(
        body,
        grid_spec=grid_spec,
        out_shape=jax.ShapeDtypeStruct((db, kvh, rpg, hd), F32),
        compiler_params=_cparams("arbitrary"),
        name="sb_decode",
    )(page_table, q, kn, vn, u, kc, vc)


def _mla_common(h_ref, m_ref, wdq_ref, gq_ref, wdkv_ref, gkv_ref, tab_ref, ckv_ref, kr_ref, rank, rope):
    u = _modulate(h_ref, m_ref)
    cq = (_rms(_dot(u, wdq_ref[...])) * gq_ref[...]).astype(BF16)
    kv = _dot(u, wdkv_ref[...])
    ckv = _rms(kv[:, :rank]) * gkv_ref[...]
    ckv_ref[...] = ckv
    hp = HEAD_PAD
    kra = kv[:, rank:rank + hp] * tab_ref[2] + kv[:, rank + hp:rank + 2 * hp] * tab_ref[3]
    kr_ref[...] = kra[:, :rope]
    return cq, kv, ckv


def _mla_proj_prompt_kernel(h_ref, m_ref, tab_ref, wdq_ref, gq_ref, wuq1_ref, wuq2_ref, wdkv_ref, gkv_ref,
                            wuk_ref, wuv_ref, q_ref, k_ref, v_ref, ckv_ref, kr_ref, *, heads, rank, rope, scale):
    cq, kv, ckv = _mla_common(h_ref, m_ref, wdq_ref, gq_ref, wdkv_ref, gkv_ref, tab_ref, ckv_ref, kr_ref, rank, rope)
    hp = HEAD_PAD
    cb, sb = tab_ref[0], tab_ref[1]
    y1 = _dot(cq, wuq1_ref[...])
    y2 = _dot(cq, wuq2_ref[...])
    krb = kv[:, rank + 2 * hp:rank + 3 * hp] * cb + kv[:, rank + 3 * hp:rank + 4 * hp] * sb
    c16 = ckv.astype(BF16)
    kn = _dot(c16, wuk_ref[...])
    v_ref[...] = _dot(c16, wuv_ref[...]).astype(v_ref.dtype)
    for hh in range(heads):
        sl = slice(hh * hp, (hh + 1) * hp)
        q_ref[:, sl] = ((y1[:, sl] * cb + y2[:, sl] * sb) * scale).astype(q_ref.dtype)
        k_ref[:, sl] = (kn[:, sl] + krb).astype(k_ref.dtype)


def _mla_proj_sample_kernel(h_ref, m_ref, tab_ref, tabq_ref, wdq_ref, gq_ref, wuqn_ref, wuqr_ref, wuqs_ref,
                            wdkv_ref, gkv_ref, wukt_ref, ql_ref, qr_ref, ckv_ref, kr_ref,
                            *, heads, rank, rope, scale):
    cq, _, _ = _mla_common(h_ref, m_ref, wdq_ref, gq_ref, wdkv_ref, gkv_ref, tab_ref, ckv_ref, kr_ref, rank, rope)
    hp = HEAD_PAD
    qn = _dot(cq, wuqn_ref[...]).astype(BF16)
    for hh in range(heads):
        ql_ref[:, hh * rank:(hh + 1) * rank] = (_dot(qn[:, hh * hp:(hh + 1) * hp], wukt_ref[hh]) * scale).astype(ql_ref.dtype)
    qr = _dot(cq, wuqr_ref[...]) * tabq_ref[0] + _dot(cq, wuqs_ref[...]) * tabq_ref[1]
    qr_ref[...] = (qr * scale).astype(qr_ref.dtype)


def _mla_attn_kernel(q_ref, k_ref, v_ref, o_ref, m_ref, l_ref, acc_ref, *, tq):
    qi = pl.program_id(2)
    q = q_ref[...]
    rep = tq // HEAD_PAD

    def scores(kb):
        start = pl.multiple_of(kb * tq, tq)
        s = lax.dot_general(q, k_ref[pl.ds(start, tq), :], _NT, preferred_element_type=F32)
        return s, v_ref[pl.ds(start, tq), :]

    s, vblk = scores(qi)
    row = lax.broadcasted_iota(jnp.int32, (tq, tq), 0)
    col = lax.broadcasted_iota(jnp.int32, (tq, tq), 1)
    s = jnp.where(col <= row, s, NEG_INF)
    m0 = jnp.max(s, axis=-1, keepdims=True)
    p = jnp.exp(s - m0)
    m_ref[...] = jnp.broadcast_to(m0, m_ref.shape)
    l_ref[...] = jnp.broadcast_to(jnp.sum(p, axis=-1, keepdims=True), l_ref.shape)
    acc_ref[...] = _dot(p.astype(BF16), vblk)

    def body(kb, carry):
        s, vblk = scores(kb)
        m_prev = m_ref[...]
        m_new = jnp.maximum(m_prev, jnp.max(s, axis=-1, keepdims=True))
        alpha = jnp.exp(m_prev - m_new)
        p = jnp.exp(s - jnp.concatenate([m_new] * rep, axis=1))
        l_ref[...] = alpha * l_ref[...] + jnp.sum(p, axis=-1, keepdims=True)
        acc_ref[...] = alpha * acc_ref[...] + _dot(p.astype(BF16), vblk)
        m_ref[...] = m_new
        return carry

    lax.fori_loop(0, qi, body, 0)
    o_ref[...] = (acc_ref[...] / l_ref[...]).astype(o_ref.dtype)


def _mla_attention(q, k, v, heads, tq):
    b, s, _ = q.shape
    tq = min(tq, s)
    body = functools.partial(_mla_attn_kernel, tq=tq)
    blk = lambda: pl.BlockSpec((None, tq, HEAD_PAD), lambda bi, hh, i: (bi, i, hh))
    full = lambda: pl.BlockSpec((None, s, HEAD_PAD), lambda bi, hh, i: (bi, 0, hh))
    return pl.pallas_call(
        body,
        grid=(b, heads, s // tq),
        in_specs=[blk(), full(), full()],
        out_specs=blk(),
        out_shape=jax.ShapeDtypeStruct(q.shape, BF16),
        scratch_shapes=[pltpu.VMEM((tq, HEAD_PAD), F32)] * 3,
        compiler_params=_cparams("parallel", "parallel", "arbitrary"),
        name="mla_attention",
    )(q, k, v)


def _mla_decode_kernel(pt_ref, ql_ref, qr_ref, cn_ref, krn_ref, cc_ref, krc_ref, o_ref,
                       cbuf, krbuf, sem, *, layer, n_chunks, ppc, heads):
    b = pl.program_id(0)
    page_len = krbuf.shape[-1]
    rows = ql_ref.shape[0]

    def copies(j):
        slot = lax.rem(j, 2)
        out = []
        for i in range(ppc):
            pg = pt_ref[b, j * ppc + i]
            out.append(pltpu.make_async_copy(cc_ref.at[layer, pg], cbuf.at[slot, pl.ds(i * page_len, page_len)],
                                             sem.at[0, slot]))
            out.append(pltpu.make_async_copy(krc_ref.at[layer, pg], krbuf.at[slot, i], sem.at[1, slot]))
        return out

    def start(j):
        for c in copies(j):
            c.start()

    def wait(j):
        for c in copies(j):
            c.wait()

    start(0)
    if n_chunks > 1:
        start(1)
    ql = ql_ref[...].astype(BF16)
    qr = qr_ref[...].astype(BF16)

    c0 = cn_ref[...].astype(BF16)
    s = lax.dot_general(ql, c0, _NT, preferred_element_type=F32) + _dot(qr, krn_ref[...].astype(BF16))
    row = lax.broadcasted_iota(jnp.int32, (rows, page_len), 0)
    col = lax.broadcasted_iota(jnp.int32, (rows, page_len), 1)
    s = jnp.where(col * heads <= row, s, NEG_INF)
    m = jnp.max(s, axis=-1, keepdims=True)
    p = jnp.exp(s - m)
    l = jnp.sum(p, axis=-1, keepdims=True)
    acc = _dot(p.astype(BF16), c0)

    def body(j, carry):
        m, l, acc = carry
        slot = lax.rem(j, 2)
        wait(j)
        c = cbuf[slot].astype(BF16)
        s = lax.dot_general(ql, c, _NT, preferred_element_type=F32)
        s = s + jnp.concatenate([_dot(qr, krbuf[slot, i].astype(BF16)) for i in range(ppc)], axis=1)
        m_new = jnp.maximum(m, jnp.max(s, axis=-1, keepdims=True))
        alpha = jnp.exp(m - m_new)
        p = jnp.exp(s - m_new)
        l = alpha * l + jnp.sum(p, axis=-1, keepdims=True)
        acc = alpha * acc + _dot(p.astype(BF16), c)

        @pl.when(j + 2 < n_chunks)
        def _():
            start(j + 2)

        return m_new, l, acc

    m, l, acc = lax.fori_loop(0, n_chunks, body, (m, l, acc))
    o_ref[...] = (acc / l).astype(o_ref.dtype)


def _mla_decode(ql, qr, cn, krn, cc, krc, page_table, layer, heads):
    db, rows, rank = ql.shape
    rope = qr.shape[-1]
    n_pages = page_table.shape[1]
    page_len = cc.shape[2]
    ppc = MLA_PAGES_PER_CHUNK if n_pages % MLA_PAGES_PER_CHUNK == 0 else 1
    n_chunks = n_pages // ppc
    body = functools.partial(_mla_decode_kernel, layer=layer, n_chunks=n_chunks, ppc=ppc, heads=heads)
    grid_spec = pltpu.PrefetchScalarGridSpec(
        num_scalar_prefetch=1,
        grid=(db,),
        in_specs=[
            pl.BlockSpec((None, rows, rank), lambda bi, pt: (bi, 0, 0)),
            pl.BlockSpec((None, rows, rope), lambda bi, pt: (bi, 0, 0)),
            pl.BlockSpec((None, page_len, rank), lambda bi, pt: (bi, 0, 0)),
            pl.BlockSpec((None, rope, page_len), lambda bi, pt: (bi, 0, 0)),
            pl.BlockSpec(memory_space=pl.ANY),
            pl.BlockSpec(memory_space=pl.ANY),
        ],
        out_specs=pl.BlockSpec((None, rows, rank), lambda bi, pt: (bi, 0, 0)),
        scratch_shapes=[
            pltpu.VMEM((2, ppc * page_len, rank), F32),
            pltpu.VMEM((2, ppc, rope, page_len), F32),
            pltpu.SemaphoreType.DMA((2, 2)),
        ],
    )
    return pl.pallas_call(
        body,
        grid_spec=grid_spec,
        out_shape=jax.ShapeDtypeStruct((db, rows, rank), BF16),
        compiler_params=_cparams("arbitrary"),
        name="mla_decode",
    )(page_table, ql, qr, cn, krn, cc, krc)


def _pad_last(x, width):
    return jnp.pad(x, [(0, 0)] * (x.ndim - 1) + [(0, width - x.shape[-1])])


def _pad_heads(w, heads):
    lead = w.shape[:-1]
    return _pad_last(w.reshape(lead + (heads, -1)), HEAD_PAD).reshape(lead + (heads * HEAD_PAD,))


def _pad_head_rows(w, heads):
    n = w.shape[-1]
    w3 = w.reshape(heads, -1, n)
    return jnp.pad(w3, ((0, 0), (0, HEAD_PAD - w3.shape[1]), (0, 0))).reshape(heads * HEAD_PAD, n)


def _rope_tables(pos, rope, nope):
    half = rope // 2
    inv_freq = jnp.exp(jnp.arange(half, dtype=F32) * (-math.log(ROPE_THETA) / half))
    ang = pos.astype(F32)[:, None] * inv_freq[None, :]
    cos, sin = jnp.cos(ang), jnp.sin(ang)
    cc = jnp.concatenate([cos, cos], axis=1)
    ss = jnp.concatenate([-sin, sin], axis=1)
    t = pos.shape[0]
    cb = jnp.concatenate([jnp.ones((t, nope), F32), cc, jnp.zeros((t, HEAD_PAD - nope - rope), F32)], axis=1)
    sb = jnp.concatenate([jnp.zeros((t, nope), F32), ss, jnp.zeros((t, HEAD_PAD - nope - rope), F32)], axis=1)
    return jnp.stack([cb, sb, _pad_last(cc, HEAD_PAD), _pad_last(ss, HEAD_PAD)]), cc, ss


def _swap_halves(x):
    half = x.shape[-1] // 2
    return jnp.concatenate([x[..., half:], x[..., :half]], axis=-1)


def _sb_layer(hp, hs, mod_p, mod_s, gp, w_qkv, w_o, kc, vc, page_table, layer, kvh, hd, dec_shape):
    b, s, d = hp.shape
    db, dt = dec_shape
    heads = w_qkv.shape[1] // hd - 2 * kvh
    grp = heads // kvh
    nq, nk = heads * hd, kvh * hd
    wq = w_qkv[:, :nq] * (hd ** -0.5)
    wk = w_qkv[:, nq:nq + nk]
    wv = w_qkv[:, nq + nk:]
    w_p = jnp.concatenate([_pad_heads(wq, heads), wk, wv, _pad_heads(wk, kvh), _pad_heads(wv, kvh)], axis=1).astype(BF16)
    w_s = jnp.concatenate([wq, wk, wv], axis=1).astype(BF16)
    wo_pad = _pad_head_rows(w_o, heads).astype(BF16)

    q_p, k_p, v_p, k_pad, v_pad = _proj(
        hp, mod_p, 1, w_p, [heads * HEAD_PAD, nk, nk, kvh * HEAD_PAD, kvh * HEAD_PAD],
        [BF16, F32, F32, BF16, BF16], PROJ_TOKENS)
    o_p = _sb_attention(q_p, k_pad, v_pad, kvh, SB_Q_TILE)
    hp = _oproj(hp, mod_p, 1, o_p, gp, wo_pad, PROJ_TOKENS)

    t_s = hs.shape[1]
    q_s, k_s, v_s = _proj(hs, mod_s, 1, w_s, [nq, nk, nk], [F32, F32, F32], t_s)
    page_len = kc.shape[2]
    q5 = q_s.reshape(db, dt, kvh, grp, hd).transpose(0, 2, 3, 1, 4).reshape(db, kvh, grp * dt, hd)
    kn = _pad_last(k_s.reshape(db, dt, kvh, hd).transpose(0, 2, 3, 1), page_len)
    vn = _pad_last(v_s.reshape(db, dt, kvh, hd).transpose(0, 2, 3, 1), page_len)
    kct = kc.transpose(0, 1, 3, 4, 2)
    vct = vc.transpose(0, 1, 3, 4, 2)
    o5 = _sb_decode(q5, kn, vn, kct, vct, page_table, layer, dt)
    o_s = o5.reshape(db, kvh, grp, dt, hd).transpose(0, 3, 1, 2, 4).reshape(1, db * dt, nq).astype(BF16)
    hs = _oproj(hs, mod_s, 1, o_s, gp, w_o.astype(BF16), t_s)
    return (hp, hs, k_p.reshape(b, s, kvh, hd), v_p.reshape(b, s, kvh, hd),
            k_s.reshape(db, dt, kvh, hd), v_s.reshape(db, dt, kvh, hd))


def _mla_layer(hp, hs, mod_p, mod_s, gp, w_dq, g_q, w_uq, w_dkv, g_kv, w_uk, w_uv, w_o,
               cc, krc, page_table, layer, dec_shape):
    b, s, d = hp.shape
    db, dt = dec_shape
    rank, heads, nope = w_uk.shape
    vdim = w_uv.shape[2]
    rope = krc.shape[-1]
    rq = w_dq.shape[1]
    scale = float((nope + rope) ** -0.5)
    hp_ = HEAD_PAD

    uq = w_uq.reshape(rq, heads, nope + rope)
    uq_n, uq_r = uq[..., :nope], uq[..., nope:]
    zeros = lambda *sh: jnp.zeros(sh, F32)
    wuq1 = _pad_last(uq, hp_).reshape(rq, heads * hp_).astype(BF16)
    wuq2 = _pad_last(jnp.concatenate([zeros(rq, heads, nope), _swap_halves(uq_r)], -1), hp_).reshape(rq, heads * hp_).astype(BF16)
    wuqn = _pad_last(uq_n, hp_).reshape(rq, heads * hp_).astype(BF16)
    wuqr = uq_r.reshape(rq, heads * rope).astype(BF16)
    wuqs = _swap_halves(uq_r).reshape(rq, heads * rope).astype(BF16)
    w_c, w_r = w_dkv[:, :rank], w_dkv[:, rank:]
    w_rs = _swap_halves(w_r)
    lead = zeros(d, nope)
    wdkv = jnp.concatenate([
        w_c, _pad_last(w_r, hp_), _pad_last(w_rs, hp_),
        _pad_last(jnp.concatenate([lead, w_r], 1), hp_), _pad_last(jnp.concatenate([lead, w_rs], 1), hp_)],
        axis=1).astype(BF16)
    wuk = _pad_last(w_uk, hp_).reshape(rank, heads * hp_).astype(BF16)
    wuv = _pad_last(w_uv, hp_).reshape(rank, heads * hp_).astype(BF16)
    wukt = jnp.pad(w_uk.transpose(1, 2, 0), ((0, 0), (0, hp_ - nope), (0, 0))).astype(BF16)
    wuvh = _pad_last(w_uv.transpose(1, 0, 2), hp_).astype(BF16)
    wo_pad = _pad_head_rows(w_o, heads).astype(BF16)
    wdq = w_dq.astype(BF16)
    gq = g_q.reshape(1, rq)
    gkv = g_kv.reshape(1, rank)

    tab_p, _, _ = _rope_tables(jnp.arange(s), rope, nope)
    tm = min(PROJ_TOKENS, s)
    body = functools.partial(_mla_proj_prompt_kernel, heads=heads, rank=rank, rope=rope, scale=scale)
    weights = [wdq, gq, wuq1, wuq2, wdkv, gkv, wuk, wuv]
    q_p, k_p, v_p, ckv_p, kr_p = _token_call(
        body, hp, mod_p, 1, [tab_p] + weights,
        [pl.BlockSpec((4, tm, hp_), lambda gi, i: (0, i, 0))] + [_const_spec(w.shape) for w in weights],
        [heads * hp_, heads * hp_, heads * hp_, rank, rope], [BF16, BF16, BF16, F32, F32], tm, "mla_proj")
    o_p = _mla_attention(q_p, k_p, v_p, heads, MLA_Q_TILE)
    hp = _oproj(hp, mod_p, 1, o_p, gp, wo_pad, PROJ_TOKENS)

    t_s = hs.shape[1]
    n_pages = page_table.shape[1]
    page_len = cc.shape[2]
    pos_s = n_pages * page_len + (jnp.arange(t_s) % dt)
    tab_s, cc_s, ss_s = _rope_tables(pos_s, rope, nope)
    tabq = jnp.stack([jnp.tile(cc_s, (1, heads)), jnp.tile(ss_s, (1, heads))])
    body = functools.partial(_mla_proj_sample_kernel, heads=heads, rank=rank, rope=rope, scale=scale)
    weights = [wdq, gq, wuqn, wuqr, wuqs, wdkv, gkv, wukt]
    ql, qr, ckv_s, kr_s = _token_call(
        body, hs, mod_s, 1, [tab_s, tabq] + weights,
        [pl.BlockSpec((4, t_s, hp_), lambda gi, i: (0, i, 0)),
         pl.BlockSpec((2, t_s, heads * rope), lambda gi, i: (0, i, 0))] + [_const_spec(w.shape) for w in weights],
        [heads * rank, heads * rope, rank, rope], [F32, F32, F32, F32], t_s, "mla_proj_s")
    ql = ql.reshape(db, dt * heads, rank)
    qr = qr.reshape(db, dt * heads, rope)
    cn = jnp.pad(ckv_s.reshape(db, dt, rank), ((0, 0), (0, page_len - dt), (0, 0)))
    krn = _pad_last(kr_s.reshape(db, dt, rope).transpose(0, 2, 1), page_len)
    krct = krc.transpose(0, 1, 3, 2)
    o_lat = _mla_decode(ql, qr, cn, krn, cc, krct, page_table, layer, heads)
    hs = _oproj_lat(hs, mod_s, 1, o_lat.reshape(1, t_s, heads * rank), gp, wuvh, wo_pad, t_s)
    return (hp, hs, ckv_p, kr_p, ckv_s.reshape(db, dt, rank), kr_s.reshape(db, dt, rope))


def kernel(x_prompt, x_sample, c_prompt, c_sample, cache_sb_k, cache_sb_v, cache_mla_ckv, cache_mla_krope, page_table, w_ada, b_ada, g_post, w_ffn_in, w_ffn_out, sb_w_qkv, sb_w_o, mla_w_dq, mla_g_q, mla_w_uq, mla_w_dkv, mla_g_kv, mla_w_uk, mla_w_uv, mla_w_o):
    b, s, d = x_prompt.shape
    db, dt, _ = x_sample.shape
    depth = w_ada.shape[0]
    kvh, hd = cache_sb_k.shape[3:]

    n_c = b + db
    n_c_pad = -(-n_c // 16) * 16
    c_all = jnp.concatenate([c_prompt, c_sample, jnp.zeros((n_c_pad - n_c, d), F32)], axis=0)
    mod = _modulation(c_all, w_ada, b_ada).reshape(depth, n_c_pad, N_MOD, d)

    hp = x_prompt
    hs = x_sample.reshape(1, db * dt, d)
    t_s = db * dt
    sbk_p, sbv_p, ckv_p, kr_p, sbk_s, sbv_s, ckv_s, kr_s = ([] for _ in range(8))
    for layer in range(depth):
        mod_p = mod[layer, :b].reshape(b, N_MOD, 1, d)
        mod_s = jnp.repeat(mod[layer, b:n_c], dt, axis=0).transpose(1, 0, 2)[None]
        g = g_post[layer]
        w_in = w_ffn_in[layer].astype(BF16)
        w_out = w_ffn_out[layer].astype(BF16)
        hp = _ffn(hp, mod_p, 0, g[0], w_in[0], w_out[0], FFN_TOKENS)
        hs = _ffn(hs, mod_s, 0, g[0], w_in[0], w_out[0], t_s)
        idx = layer // 2
        if layer % 2 == 0:
            hp, hs, kp, vp, ks, vs = _sb_layer(
                hp, hs, mod_p, mod_s, g[1], sb_w_qkv[idx], sb_w_o[idx], cache_sb_k, cache_sb_v,
                page_table, idx, kvh, hd, (db, dt))
            sbk_p.append(kp)
            sbv_p.append(vp)
            sbk_s.append(ks)
            sbv_s.append(vs)
        else:
            hp, hs, cp, rp, cs, rs = _mla_layer(
                hp, hs, mod_p, mod_s, g[1], mla_w_dq[idx], mla_g_q[idx], mla_w_uq[idx], mla_w_dkv[idx],
                mla_g_kv[idx], mla_w_uk[idx], mla_w_uv[idx], mla_w_o[idx], cache_mla_ckv, cache_mla_krope,
                page_table, idx, (db, dt))
            ckv_p.append(cp)
            kr_p.append(rp)
            ckv_s.append(cs)
            kr_s.append(rs)
        hp = _ffn(hp, mod_p, 2, g[2], w_in[1], w_out[1], FFN_TOKENS)
        hs = _ffn(hs, mod_s, 2, g[2], w_in[1], w_out[1], t_s)
    return (hp, hs.reshape(db, dt, d), jnp.stack(sbk_p), jnp.stack(sbv_p), jnp.stack(ckv_p), jnp.stack(kr_p),
            jnp.stack(sbk_s), jnp.stack(sbv_s), jnp.stack(ckv_s), jnp.stack(kr_s))
```
